```python
import jax, jax.numpy as jnp
from jax import lax
import numpy as np

D_MODEL = 1024
BATCH = 2
SEQ = 8192
DEPTH = 4

D_MIX = D_MODEL
A_WIDTH = D_MIX // 2
B_WIDTH = D_MIX - A_WIDTH
A_HEADS = 4
A_HEAD_DIM = A_WIDTH // A_HEADS
CHUNK = 128
B_HEADS = 4
B_VAL_DIM = B_WIDTH // B_HEADS
B_KEY_DIM = B_VAL_DIM // 2
QK_WIDTH = B_HEADS * B_KEY_DIM
GATE_RANK = 16
GATE_NORMALIZER = 16.0
N_IN = 2 * A_WIDTH + 2 * QK_WIDTH + 2 * B_WIDTH + 2 * GATE_RANK
D_FF = 2816
CONV_WIDTH = 3
EPS = 1e-6

kernel_name = "hybrid_gmlp_gla_convffn_encoder"


def _split_points():
    sizes = [A_WIDTH, A_WIDTH, QK_WIDTH, QK_WIDTH, B_WIDTH, B_WIDTH, GATE_RANK, GATE_RANK]
    pts, acc = [], 0
    for s in sizes[:-1]:
        acc += s
        pts.append(acc)
    return pts


def _rmsnorm(x, g):
    xf = x.astype(jnp.float32)
    y = xf * lax.rsqrt(jnp.mean(xf * xf, axis=-1, keepdims=True) + EPS)
    return (y * g.astype(jnp.float32)).astype(x.dtype)


def _layernorm(x, g, b):
    xf = x.astype(jnp.float32)
    mu = jnp.mean(xf, axis=-1, keepdims=True)
    var = jnp.mean(jnp.square(xf - mu), axis=-1, keepdims=True)
    y = (xf - mu) * lax.rsqrt(var + EPS)
    return (y * g.astype(jnp.float32) + b.astype(jnp.float32)).astype(x.dtype)


def _spatial_gating(u, v, w_s, b_s, ln_g, ln_b):
    bsz, s = u.shape[0], u.shape[1]
    n = s // CHUNK
    vn = _layernorm(v, ln_g, ln_b).reshape(bsz, n, CHUNK, A_HEADS, A_HEAD_DIM)
    mixed = jnp.einsum('hij,bnjhd->bnihd', w_s, vn) + b_s.T[None, None, :, :, None]
    return u * mixed.reshape(bsz, s, A_HEADS, A_HEAD_DIM)


def _gla_chunked(q, k, v, log_a, include_diag):
    bsz, s, h, dk = q.shape
    dv = v.shape[-1]
    n = s // CHUNK
    f32 = jnp.float32
    qc = q.astype(f32).reshape(bsz, n, CHUNK, h, dk)
    kc = k.astype(f32).reshape(bsz, n, CHUNK, h, dk)
    vc = v.astype(f32).reshape(bsz, n, CHUNK, h, dv)
    cum = jnp.cumsum(log_a.astype(f32).reshape(bsz, n, CHUNK, h, dk), axis=2)
    cum_last = cum[:, :, -1:]
    q_dec = qc * jnp.exp(cum)
    k_inv = kc * jnp.exp(-cum)
    k_to_end = kc * jnp.exp(cum_last - cum)
    scores = jnp.einsum('bnihd,bnjhd->bnhij', q_dec, k_inv)
    mask = jnp.tril(jnp.ones((CHUNK, CHUNK), dtype=bool), k=0 if include_diag else -1)
    scores = jnp.where(mask, scores, 0.0)
    o_intra = jnp.einsum('bnhij,bnjhe->bnihe', scores, vc)
    d_state = jnp.einsum('bnjhd,bnjhe->bnhde', k_to_end, vc)
    chunk_decay = jnp.exp(cum_last[:, :, 0])

    def step(state, inp):
        ds, dec = inp
        return state * dec[..., None] + ds, state

    s0 = jnp.zeros((bsz, h, dk, dv), f32)
    _, states = lax.scan(step, s0, (jnp.swapaxes(d_state, 0, 1), jnp.swapaxes(chunk_decay, 0, 1)))
    states = jnp.swapaxes(states, 0, 1)
    o_inter = jnp.einsum('bnihd,bnhde->bnihe', q_dec, states)
    return (o_intra + o_inter).reshape(bsz, s, h, dv)


def _dwconv_centred(z, w, b):
    s = z.shape[1]
    half = CONV_WIDTH // 2
    zp = jnp.pad(z, ((0, 0), (half, half), (0, 0)))
    out = b
    for t in range(CONV_WIDTH):
        out = out + zp[:, t:t + s] * w[t]
    return out


def setup_inputs(seed: int = 0) -> dict:
    key = jax.random.key(seed)
    ks = jax.random.split(key, 20)
    f32 = jnp.float32
    nrm = lambda k, shape, scale: jax.random.normal(k, shape, f32) * scale
    return {
        "x": nrm(ks[0], (BATCH, SEQ, D_MODEL), 1.0),
        "g_mix": 1.0 + nrm(ks[1], (DEPTH, D_MODEL), 0.02),
        "w_in": nrm(ks[2], (DEPTH, D_MODEL, N_IN), D_MODEL ** -0.5),
        "w_s": nrm(ks[3], (DEPTH, A_HEADS, CHUNK, CHUNK), CHUNK ** -0.5),
        "b_s": 1.0 + nrm(ks[4], (DEPTH, A_HEADS, CHUNK), 0.1),
        "ln_g": 1.0 + nrm(ks[5], (DEPTH, A_HEADS, A_HEAD_DIM), 0.02),
        "ln_b": nrm(ks[6], (DEPTH, A_HEADS, A_HEAD_DIM), 0.02),
        "w_gate_f": nrm(ks[7], (DEPTH, GATE_RANK, QK_WIDTH), GATE_RANK ** -0.5),
        "b_gate_f": nrm(ks[8], (DEPTH, QK_WIDTH), 0.1),
        "w_gate_b": nrm(ks[9], (DEPTH, GATE_RANK, QK_WIDTH), GATE_RANK ** -0.5),
        "b_gate_b": nrm(ks[10], (DEPTH, QK_WIDTH), 0.1),
        "g_gla": 1.0 + nrm(ks[11], (DEPTH, B_HEADS, B_VAL_DIM), 0.02),
        "w_out": nrm(ks[12], (DEPTH, D_MIX, D_MODEL), D_MIX ** -0.5),
        "g_ffn": 1.0 + nrm(ks[13], (DEPTH, D_MODEL), 0.02),
        "w_up": nrm(ks[14], (DEPTH, D_MODEL, 2 * D_FF), D_MODEL ** -0.5),
        "conv_w": nrm(ks[15], (DEPTH, CONV_WIDTH, 2 * D_FF), CONV_WIDTH ** -0.5),
        "conv_b": nrm(ks[16], (DEPTH, 2 * D_FF), 0.02),
        "w_down": nrm(ks[17], (DEPTH, D_FF, D_MODEL), D_FF ** -0.5),
        "g_final": 1.0 + nrm(ks[18], (D_MODEL,), 0.02),
    }


def reference(x, g_mix, w_in, w_s, b_s, ln_g, ln_b, w_gate_f, b_gate_f, w_gate_b, b_gate_b,
              g_gla, w_out, g_ffn, w_up, conv_w, conv_b, w_down, g_final):
    bsz, s, _ = x.shape
    pts = _split_points()
    q_scale = B_KEY_DIM ** -0.5
    for l in range(DEPTH):
        h = _rmsnorm(x, g_mix[l])
        p = h @ w_in[l]
        pa_u, pa_v, pq, pk, pv, pg, r_f, r_b = jnp.split(p, pts, axis=-1)
        u = jax.nn.gelu(pa_u, approximate=False).reshape(bsz, s, A_HEADS, A_HEAD_DIM)
        va = jax.nn.gelu(pa_v, approximate=False).reshape(bsz, s, A_HEADS, A_HEAD_DIM)
        out_a = _spatial_gating(u, va, w_s[l], b_s[l], ln_g[l], ln_b[l])
        q = pq.reshape(bsz, s, B_HEADS, B_KEY_DIM) * q_scale
        k = pk.reshape(bsz, s, B_HEADS, B_KEY_DIM)
        vb = pv.reshape(bsz, s, B_HEADS, B_VAL_DIM)
        la_f = (jax.nn.log_sigmoid((r_f @ w_gate_f[l] + b_gate_f[l]).astype(jnp.float32))
                / GATE_NORMALIZER).reshape(bsz, s, B_HEADS, B_KEY_DIM)
        la_b = (jax.nn.log_sigmoid((r_b @ w_gate_b[l] + b_gate_b[l]).astype(jnp.float32))
                / GATE_NORMALIZER).reshape(bsz, s, B_HEADS, B_KEY_DIM)
        o_fwd = _gla_chunked(q, k, vb, la_f, True)
        o_bwd = jnp.flip(_gla_chunked(jnp.flip(q, 1), jnp.flip(k, 1), jnp.flip(vb, 1),
                                      jnp.flip(la_b, 1), False), 1)
        o = (o_fwd + o_bwd).astype(x.dtype)
        out_b = _rmsnorm(o, g_gla[l]) * jax.nn.silu(pg.reshape(bsz, s, B_HEADS, B_VAL_DIM))
        mixed = jnp.concatenate([out_a.reshape(bsz, s, A_WIDTH),
                                 out_b.reshape(bsz, s, B_WIDTH)], axis=-1)
        x = x + mixed @ w_out[l]
        h = _rmsnorm(x, g_ffn[l])
        z = _dwconv_centred(h @ w_up[l], conv_w[l], conv_b[l])
        z_gate, z_val = jnp.split(z, [D_FF], axis=-1)
        x = x + (jax.nn.silu(z_gate) * z_val) @ w_down[l]
    return _rmsnorm(x, g_final)
```

```python
import functools

import jax
import jax.numpy as jnp
from jax import lax
from jax.experimental import pallas as pl
from jax.experimental.pallas import tpu as pltpu

CHUNK = 128
A_HEADS = 4
A_HEAD_DIM = 128
B_HEADS = 4
B_KEY_DIM = 64
B_VAL_DIM = 128
GATE_RANK = 16
GATE_NORMALIZER = 16.0
CONV_WIDTH = 3
EPS = 1e-6

A_WIDTH = A_HEADS * A_HEAD_DIM
QK_WIDTH = B_HEADS * B_KEY_DIM
B_WIDTH = B_HEADS * B_VAL_DIM

LANES = 128
SUBLANES = 8
VMEM_LIMIT_BYTES = 56 * 1024 * 1024

_OFF_U = 0
_OFF_V = _OFF_U + A_WIDTH
_OFF_Q = _OFF_V + A_WIDTH
_OFF_K = _OFF_Q + QK_WIDTH
_OFF_VB = _OFF_K + QK_WIDTH
_OFF_G = _OFF_VB + B_WIDTH
_OFF_R = _OFF_G + B_WIDTH
N_IN_PAD = _OFF_R + LANES

FF_BLOCK = 256

_NT = (((1,), (1,)), ((), ()))


def _dot(a, b):
    return jnp.dot(a, b, preferred_element_type=jnp.float32)


def _dot_nt(a, b):
    return lax.dot_general(a, b, _NT, preferred_element_type=jnp.float32)


def _bf16(a):
    return a.astype(jnp.bfloat16)


def _rms_scale(x):
    return lax.rsqrt(jnp.mean(x * x, axis=-1, keepdims=True) + EPS)


def _gelu(x):
    return 0.5 * x * (1.0 + lax.erf(x * (2.0 ** -0.5)))


def _log_sigmoid(x):
    return jnp.minimum(x, 0.0) - jnp.log1p(jnp.exp(-jnp.abs(x)))


def _split_hi_lo(a):
    hi = _bf16(a)
    lo = _bf16(a - hi.astype(jnp.float32))
    return hi, lo


def _head_lane_mask(h):
    lane = lax.broadcasted_iota(jnp.int32, (CHUNK, LANES), 1)
    return (lane >= B_KEY_DIM) if (h % 2) else (lane < B_KEY_DIM)


def _mixer_fwd_kernel(x_ref, gmix_ref, win_ref, ws_ref, bs_ref, lng_ref, lnb_ref, wg_ref, bg_ref,
                      outa_ref, oa_ref, qdb_ref, gate_ref, dsb_ref, decb_ref,
                      p_ref, sf_ref, *, tiles_per_seq, n_chunks):
    @pl.when(pl.program_id(0) % tiles_per_seq == 0)
    def _():
        sf_ref[...] = jnp.zeros_like(sf_ref)

    x = x_ref[...]
    h = _bf16(x * _rms_scale(x) * gmix_ref[...])
    p_ref[...] = _dot(h, win_ref[...])

    row = lax.broadcasted_iota(jnp.int32, (CHUNK, CHUNK), 0)
    col = lax.broadcasted_iota(jnp.int32, (CHUNK, CHUNK), 1)
    lower_incl = row >= col
    tril = _bf16(jnp.where(lower_incl, 1.0, 0.0))
    triu = _bf16(jnp.where(row <= col, 1.0, 0.0))
    q_scale = B_KEY_DIM ** -0.5

    def chunk_body(c, carry):
        r0 = pl.multiple_of(c * CHUNK, CHUNK)
        rows = pl.ds(r0, CHUNK)

        for hd in range(A_HEADS):
            cols = slice(hd * A_HEAD_DIM, (hd + 1) * A_HEAD_DIM)
            u = _gelu(p_ref[rows, _OFF_U + hd * A_HEAD_DIM:_OFF_U + (hd + 1) * A_HEAD_DIM])
            v = _gelu(p_ref[rows, _OFF_V + hd * A_HEAD_DIM:_OFF_V + (hd + 1) * A_HEAD_DIM])
            mu = jnp.mean(v, axis=-1, keepdims=True)
            d = v - mu
            var = jnp.mean(d * d, axis=-1, keepdims=True)
            vn = d * lax.rsqrt(var + EPS) * lng_ref[:, cols] + lnb_ref[:, cols]
            mixed = _dot(ws_ref[hd], _bf16(vn)) + bs_ref[hd]
            outa_ref[rows, cols] = _bf16(u * mixed)

        q = p_ref[rows, _OFF_Q:_OFF_Q + QK_WIDTH] * q_scale
        k = p_ref[rows, _OFF_K:_OFF_K + QK_WIDTH]
        vb = _bf16(p_ref[rows, _OFF_VB:_OFF_VB + B_WIDTH])
        pg = p_ref[rows, _OFF_G:_OFF_G + B_WIDTH]
        r = _bf16(p_ref[rows, _OFF_R:_OFF_R + LANES])
        gate_ref[rows, :] = _bf16(pg * jax.nn.sigmoid(pg))

        la = _log_sigmoid(_dot(r, wg_ref[...]) + bg_ref[...]) * (1.0 / GATE_NORMALIZER)
        la_f_hi, la_f_lo = _split_hi_lo(la[:, :QK_WIDTH])
        la_b_hi, la_b_lo = _split_hi_lo(la[:, QK_WIDTH:])
        cum_f = _dot(tril, la_f_hi) + _dot(tril, la_f_lo)
        cum_b = _dot(triu, la_b_hi) + _dot(triu, la_b_lo)
        tot_f = cum_f[CHUNK - 1:CHUNK, :]
        tot_b = cum_b[0:1, :]
        qd_f = q * jnp.exp(cum_f)
        ki_f = k * jnp.exp(-cum_f)
        ke_f = k * jnp.exp(tot_f - cum_f)
        qd_b = q * jnp.exp(cum_b)
        ki_b = k * jnp.exp(-cum_b)
        ke_b = k * jnp.exp(tot_b - cum_b)
        dec_f = jnp.exp(tot_f)
        decb_ref[c] = jnp.exp(tot_b)
        qdb_ref[rows, :] = _bf16(qd_b)

        vt = _bf16(p_ref[rows, _OFF_VB:_OFF_VB + B_WIDTH].T)
        ds_t = _dot(vt, _bf16(jnp.concatenate([ke_f, ke_b], axis=1)))

        for hd in range(B_HEADS):
            pair = slice((hd // 2) * LANES, (hd // 2 + 1) * LANES)
            vcols = slice(hd * B_VAL_DIM, (hd + 1) * B_VAL_DIM)
            mask = _head_lane_mask(hd)
            qf = _bf16(jnp.where(mask, qd_f[:, pair], 0.0))
            qb = _bf16(jnp.where(mask, qd_b[:, pair], 0.0))
            s_f = _dot_nt(qf, _bf16(ki_f[:, pair]))
            s_b = _dot_nt(qb, _bf16(ki_b[:, pair]))
            probs = _bf16(jnp.where(lower_incl, s_f, s_b))
            s_prev = sf_ref[hd]
            o = _dot(probs, vb[:, vcols]) + _dot_nt(qf, _bf16(s_prev))
            oa_ref[rows, vcols] = o
            sf_ref[hd] = s_prev * dec_f[:, pair] + ds_t[vcols, pair]
            dsb_ref[c, hd] = ds_t[vcols, QK_WIDTH + (hd // 2) * LANES:QK_WIDTH + (hd // 2 + 1) * LANES]
        return carry

    lax.fori_loop(0, n_chunks, chunk_body, 0)


def _mixer_fwd(x2d, gmix, win, ws, bs, lng, lnb, wg, bg, *, seq_len, tile_rows):
    t, d = x2d.shape
    n_tiles = t // tile_rows
    n_chunks = tile_rows // CHUNK
    kern = functools.partial(_mixer_fwd_kernel, tiles_per_seq=seq_len // tile_rows, n_chunks=n_chunks)
    const = lambda shape: pl.BlockSpec(shape, lambda i: (0,) * len(shape), pipeline_mode=pl.Buffered(1))
    rows = lambda width: pl.BlockSpec((tile_rows, width), lambda i: (i, 0))
    return pl.pallas_call(
        kern,
        grid=(n_tiles,),
        in_specs=[
            rows(d),
            const((1, d)),
            const((d, N_IN_PAD)),
            const((A_HEADS, CHUNK, CHUNK)),
            const((A_HEADS, CHUNK, 1)),
            const((1, A_WIDTH)),
            const((1, A_WIDTH)),
            const((LANES, 2 * QK_WIDTH)),
            const((1, 2 * QK_WIDTH)),
        ],
        out_specs=[
            rows(A_WIDTH),
            rows(B_WIDTH),
            rows(QK_WIDTH),
            rows(B_WIDTH),
            pl.BlockSpec((n_chunks, B_HEADS, B_VAL_DIM, LANES), lambda i: (i, 0, 0, 0)),
            pl.BlockSpec((n_chunks, 1, QK_WIDTH), lambda i: (i, 0, 0)),
        ],
        out_shape=[
            jax.ShapeDtypeStruct((t, A_WIDTH), jnp.bfloat16),
            jax.ShapeDtypeStruct((t, B_WIDTH), jnp.float32),
            jax.ShapeDtypeStruct((t, QK_WIDTH), jnp.bfloat16),
            jax.ShapeDtypeStruct((t, B_WIDTH), jnp.bfloat16),
            jax.ShapeDtypeStruct((t // CHUNK, B_HEADS, B_VAL_DIM, LANES), jnp.float32),
            jax.ShapeDtypeStruct((t // CHUNK, 1, QK_WIDTH), jnp.float32),
        ],
        scratch_shapes=[
            pltpu.VMEM((tile_rows, N_IN_PAD), jnp.float32),
            pltpu.VMEM((B_HEADS, B_VAL_DIM, LANES), jnp.float32),
        ],
        compiler_params=pltpu.CompilerParams(
            dimension_semantics=("arbitrary",), vmem_limit_bytes=VMEM_LIMIT_BYTES),
        name="mixer_fwd",
    )(x2d, gmix, win, ws, bs, lng, lnb, wg, bg)


def _mixer_bwd_kernel(x_ref, outa_ref, oa_ref, qdb_ref, gate_ref, dsb_ref, decb_ref, ggla_ref, wout_ref,
                      x1_ref, mixb_ref, sb_ref, *, tiles_per_seq, n_chunks):
    @pl.when(pl.program_id(0) % tiles_per_seq == 0)
    def _():
        sb_ref[...] = jnp.zeros_like(sb_ref)

    def chunk_body(step, carry):
        c = n_chunks - 1 - step
        r0 = pl.multiple_of(c * CHUNK, CHUNK)
        rows = pl.ds(r0, CHUNK)
        dec_b = decb_ref[c]
        for hd in range(B_HEADS):
            pair = slice((hd // 2) * LANES, (hd // 2 + 1) * LANES)
            vcols = slice(hd * B_VAL_DIM, (hd + 1) * B_VAL_DIM)
            qb = jnp.where(_head_lane_mask(hd), qdb_ref[rows, pair], jnp.zeros((), jnp.bfloat16))
            s_prev = sb_ref[hd]
            o = oa_ref[rows, vcols] + _dot_nt(qb, _bf16(s_prev))
            out_b = o * _rms_scale(o) * ggla_ref[:, vcols] * gate_ref[rows, vcols].astype(jnp.float32)
            mixb_ref[rows, vcols] = _bf16(out_b)
            sb_ref[hd] = s_prev * dec_b[:, pair] + dsb_ref[c, hd]
        return carry

    lax.fori_loop(0, n_chunks, chunk_body, 0)
    x1_ref[...] = (x_ref[...]
                   + _dot(outa_ref[...], wout_ref[:A_WIDTH, :])
                   + _dot(mixb_ref[...], wout_ref[A_WIDTH:, :]))


def _mixer_bwd(x2d, outa, oa, qdb, gate, dsb, decb, ggla, wout, *, seq_len, tile_rows):
    t, d = x2d.shape
    n_tiles = t // tile_rows
    n_chunks = tile_rows // CHUNK
    kern = functools.partial(_mixer_bwd_kernel, tiles_per_seq=seq_len // tile_rows, n_chunks=n_chunks)
    const = lambda shape: pl.BlockSpec(shape, lambda i: (0,) * len(shape), pipeline_mode=pl.Buffered(1))
    rows = lambda width: pl.BlockSpec((tile_rows, width), lambda i: (n_tiles - 1 - i, 0))
    return pl.pallas_call(
        kern,
        grid=(n_tiles,),
        in_specs=[
            rows(d),
            rows(A_WIDTH),
            rows(B_WIDTH),
            rows(QK_WIDTH),
            rows(B_WIDTH),
            pl.BlockSpec((n_chunks, B_HEADS, B_VAL_DIM, LANES), lambda i: (n_tiles - 1 - i, 0, 0, 0)),
            pl.BlockSpec((n_chunks, 1, QK_WIDTH), lambda i: (n_tiles - 1 - i, 0, 0)),
            const((1, B_WIDTH)),
            const((A_WIDTH + B_WIDTH, d)),
        ],
        out_specs=rows(d),
        out_shape=jax.ShapeDtypeStruct((t, d), jnp.float32),
        scratch_shapes=[
            pltpu.VMEM((tile_rows, B_WIDTH), jnp.bfloat16),
            pltpu.VMEM((B_HEADS, B_VAL_DIM, LANES), jnp.float32),
        ],
        compiler_params=pltpu.CompilerParams(
            dimension_semantics=("arbitrary",), vmem_limit_bytes=VMEM_LIMIT_BYTES),
        name="mixer_bwd",
    )(x2d, outa, oa, qdb, gate, dsb, decb, ggla, wout)


def _ffn_kernel(x_ref, xprev_ref, xnext_ref, gffn_ref, wup_ref, cw_ref, cb_ref, wdown_ref, gfin_ref,
                out_ref, acc_ref, *, tiles_per_seq, d_ff, apply_final_norm):
    tile_rows = x_ref.shape[0]
    pos = pl.program_id(0) % tiles_per_seq
    g = gffn_ref[...]

    def normed(xv):
        return xv * _rms_scale(xv) * g

    x = x_ref[...]
    h_prev = jnp.where(pos == 0, 0.0, normed(xprev_ref[...]))
    h_next = jnp.where(pos == tiles_per_seq - 1, 0.0, normed(xnext_ref[...]))
    hcat = _bf16(jnp.concatenate([h_prev, normed(x), h_next], axis=0))
    n_cat = tile_rows + 2 * SUBLANES
    centre = slice(SUBLANES, SUBLANES + tile_rows)

    def conv(u, cols):
        w = cw_ref[:, cols]
        u_dn = pltpu.roll(u, 1, 0)
        u_up = pltpu.roll(u, n_cat - 1, 0)
        return (cb_ref[:, cols] + u_dn[centre] * w[0:1] + u[centre] * w[1:2] + u_up[centre] * w[2:3])

    for j in range(d_ff // FF_BLOCK):
        gcols = slice(j * FF_BLOCK, (j + 1) * FF_BLOCK)
        vcols = slice(d_ff + j * FF_BLOCK, d_ff + (j + 1) * FF_BLOCK)
        zg = conv(_dot(hcat, wup_ref[:, gcols]), gcols)
        zv = conv(_dot(hcat, wup_ref[:, vcols]), vcols)
        act = _bf16(zg * jax.nn.sigmoid(zg) * zv)
        contrib = _dot(act, wdown_ref[gcols, :])
        if j == 0:
            acc_ref[...] = x + contrib
        else:
            acc_ref[...] += contrib

    y = acc_ref[...]
    if apply_final_norm:
        y = y * _rms_scale(y) * gfin_ref[...]
    out_ref[...] = y


def _ffn(x2d, gffn, wup, cw, cb, wdown, gfin, *, seq_len, tile_rows, apply_final_norm):
    t, d = x2d.shape
    d_ff = wdown.shape[0]
    assert d_ff % FF_BLOCK == 0
    n_tiles = t // tile_rows
    blocks_per_tile = tile_rows // SUBLANES
    n_row_blocks = t // SUBLANES
    kern = functools.partial(_ffn_kernel, tiles_per_seq=seq_len // tile_rows, d_ff=d_ff,
                             apply_final_norm=apply_final_norm)
    const = lambda shape: pl.BlockSpec(shape, lambda i: (0,) * len(shape), pipeline_mode=pl.Buffered(1))
    return pl.pallas_call(
        kern,
        grid=(n_tiles,),
        in_specs=[
            pl.BlockSpec((tile_rows, d), lambda i: (i, 0)),
            pl.BlockSpec((SUBLANES, d), lambda i: (jnp.maximum(i * blocks_per_tile - 1, 0), 0)),
            pl.BlockSpec((SUBLANES, d), lambda i: (jnp.minimum((i + 1) * blocks_per_tile, n_row_blocks - 1), 0)),
            const((1, d)),
            const((d, 2 * d_ff)),
            const((CONV_WIDTH, 2 * d_ff)),
            const((1, 2 * d_ff)),
            const((d_ff, d)),
            const((1, d)),
        ],
        out_specs=pl.BlockSpec((tile_rows, d), lambda i: (i, 0)),
        out_shape=jax.ShapeDtypeStruct((t, d), jnp.float32),
        scratch_shapes=[pltpu.VMEM((tile_rows, d), jnp.float32)],
        compiler_params=pltpu.CompilerParams(
            dimension_semantics=("parallel",), vmem_limit_bytes=VMEM_LIMIT_BYTES),
        name="ffn_final" if apply_final_norm else "ffn",
    )(x2d, x2d, x2d, gffn, wup, cw, cb, wdown, gfin)


def _pick_tile_rows(seq_len, target):
    tile = min(target, seq_len)
    assert seq_len % tile == 0 and tile % CHUNK == 0
    return tile


def kernel(x, g_mix, w_in, w_s, b_s, ln_g, ln_b, w_gate_f, b_gate_f, w_gate_b, b_gate_b, g_gla, w_out, g_ffn, w_up, conv_w, conv_b, w_down, g_final):
    bsz, seq_len, d = x.shape
    depth = w_in.shape[0]
    assert seq_len % CHUNK == 0
    tile_rows = _pick_tile_rows(seq_len, 512)
    f32, bf16 = jnp.float32, jnp.bfloat16

    n_in = w_in.shape[-1]
    win = jnp.pad(w_in, ((0, 0), (0, 0), (0, N_IN_PAD - n_in))).astype(bf16)
    wg = jnp.zeros((depth, LANES, 2 * QK_WIDTH), f32)
    wg = wg.at[:, :GATE_RANK, :QK_WIDTH].set(w_gate_f)
    wg = wg.at[:, GATE_RANK:2 * GATE_RANK, QK_WIDTH:].set(w_gate_b)
    wg = wg.astype(bf16)
    bg = jnp.concatenate([b_gate_f, b_gate_b], axis=-1)[:, None, :]
    ws = w_s.astype(bf16)
    bs = b_s[..., None]
    lng = ln_g.reshape(depth, 1, A_WIDTH)
    lnb = ln_b.reshape(depth, 1, A_WIDTH)
    ggla = g_gla.reshape(depth, 1, B_WIDTH)
    wout = w_out.astype(bf16)
    wup = w_up.astype(bf16)
    wdown = w_down.astype(bf16)
    gfin = g_final.reshape(1, d)

    xs = x.reshape(bsz * seq_len, d)
    for l in range(depth):
        outa, oa, qdb, gate, dsb, decb = _mixer_fwd(
            xs, g_mix[l][None], win[l], ws[l], bs[l], lng[l], lnb[l], wg[l], bg[l],
            seq_len=seq_len, tile_rows=tile_rows)
        xs = _mixer_bwd(xs, outa, oa, qdb, gate, dsb, decb, ggla[l], wout[l],
                        seq_len=seq_len, tile_rows=tile_rows)
        xs = _ffn(xs, g_ffn[l][None], wup[l], conv_w[l], conv_b[l][None], wdown[l], gfin,
                  seq_len=seq_len, tile_rows=tile_rows, apply_final_norm=(l == depth - 1))
    return xs.reshape(bsz, seq_len, d)
```

```python
import functools

import jax
import jax.numpy as jnp
from jax import lax
from jax.experimental import pallas as pl
from jax.experimental.pallas import tpu as pltpu

CHUNK = 128
A_HEADS = 4
A_HEAD_DIM = 128
B_HEADS = 4
B_KEY_DIM = 64
B_VAL_DIM = 128
GATE_RANK = 16
GATE_NORMALIZER = 16.0
CONV_WIDTH = 3
EPS = 1e-6

A_WIDTH = A_HEADS * A_HEAD_DIM
QK_WIDTH = B_HEADS * B_KEY_DIM
B_WIDTH = B_HEADS * B_VAL_DIM

LANES = 128
HALO_ROWS = 8
VMEM_LIMIT_BYTES = 56 * 1024 * 1024

_OFF_U = 0
_OFF_V = _OFF_U + A_WIDTH
_OFF_Q = _OFF_V + A_WIDTH
_OFF_K = _OFF_Q + QK_WIDTH
_OFF_VB = _OFF_K + QK_WIDTH
_OFF_G = _OFF_VB + B_WIDTH
_OFF_R = _OFF_G + B_WIDTH
N_IN_PAD = _OFF_R + LANES

FF_BLOCK = 256

_NT = (((1,), (1,)), ((), ()))


def _dot(a, b):
    return jnp.dot(a, b, preferred_element_type=jnp.float32)


def _dot_nt(a, b):
    return lax.dot_general(a, b, _NT, preferred_element_type=jnp.float32)


def _bf16(a):
    return a.astype(jnp.bfloat16)


def _rms_scale(x):
    return lax.rsqrt(jnp.mean(x * x, axis=-1, keepdims=True) + EPS)


def _gelu(x):
    return 0.5 * x * (1.0 + lax.erf(x * (2.0 ** -0.5)))


def _log_sigmoid(x):
    return jnp.minimum(x, 0.0) - jnp.log1p(jnp.exp(-jnp.abs(x)))


def _split_hi_lo(a):
    hi = _bf16(a)
    lo = _bf16(a - hi.astype(jnp.float32))
    return hi, lo


def _head_lane_mask(h):
    lane = lax.broadcasted_iota(jnp.int32, (CHUNK, LANES), 1)
    return (lane >= B_KEY_DIM) if (h % 2) else (lane < B_KEY_DIM)


def _mixer_fwd_kernel(x_ref, gmix_ref, win_ref, ws_ref, bs_ref, lng_ref, lnb_ref, wg_ref, bg_ref,
                      outa_ref, oa_ref, qdb_ref, gate_ref, dsb_ref, decb_ref,
                      p_ref, sf_ref, *, tiles_per_seq, n_chunks):
    @pl.when(pl.program_id(0) % tiles_per_seq == 0)
    def _():
        sf_ref[...] = jnp.zeros_like(sf_ref)

    x = x_ref[...]
    h = _bf16(x * _rms_scale(x) * gmix_ref[...])
    p_ref[...] = _dot(h, win_ref[...])

    row = lax.broadcasted_iota(jnp.int32, (CHUNK, CHUNK), 0)
    col = lax.broadcasted_iota(jnp.int32, (CHUNK, CHUNK), 1)
    lower_incl = row >= col
    tril = _bf16(jnp.where(lower_incl, 1.0, 0.0))
    triu = _bf16(jnp.where(row <= col, 1.0, 0.0))
    q_scale = B_KEY_DIM ** -0.5

    def chunk_body(c):
        rows = pl.ds(c * CHUNK, CHUNK)

        for hd in range(A_HEADS):
            cols = slice(hd * A_HEAD_DIM, (hd + 1) * A_HEAD_DIM)
            u = _gelu(p_ref[rows, _OFF_U + hd * A_HEAD_DIM:_OFF_U + (hd + 1) * A_HEAD_DIM])
            v = _gelu(p_ref[rows, _OFF_V + hd * A_HEAD_DIM:_OFF_V + (hd + 1) * A_HEAD_DIM])
            mu = jnp.mean(v, axis=-1, keepdims=True)
            d = v - mu
            var = jnp.mean(d * d, axis=-1, keepdims=True)
            vn = d * lax.rsqrt(var + EPS) * lng_ref[:, cols] + lnb_ref[:, cols]
            mixed = _dot(ws_ref[hd], _bf16(vn)) + bs_ref[hd]
            outa_ref[rows, cols] = _bf16(u * mixed)

        q = p_ref[rows, _OFF_Q:_OFF_Q + QK_WIDTH] * q_scale
        k = p_ref[rows, _OFF_K:_OFF_K + QK_WIDTH]
        vb = _bf16(p_ref[rows, _OFF_VB:_OFF_VB + B_WIDTH])
        pg = p_ref[rows, _OFF_G:_OFF_G + B_WIDTH]
        r = _bf16(p_ref[rows, _OFF_R:_OFF_R + LANES])
        gate_ref[rows, :] = _bf16(pg * jax.nn.sigmoid(pg))

        la = _log_sigmoid(_dot(r, wg_ref[...]) + bg_ref[...]) * (1.0 / GATE_NORMALIZER)
        la_f_hi, la_f_lo = _split_hi_lo(la[:, :QK_WIDTH])
        la_b_hi, la_b_lo = _split_hi_lo(la[:, QK_WIDTH:])
        cum_f = _dot(tril, la_f_hi) + _dot(tril, la_f_lo)
        cum_b = _dot(triu, la_b_hi) + _dot(triu, la_b_lo)
        tot_f = cum_f[CHUNK - 1:CHUNK, :]
        tot_b = cum_b[0:1, :]
        qd_f = q * jnp.exp(cum_f)
        ki_f = k * jnp.exp(-cum_f)
        ke_f = k * jnp.exp(tot_f - cum_f)
        qd_b = q * jnp.exp(cum_b)
        ki_b = k * jnp.exp(-cum_b)
        ke_b = k * jnp.exp(tot_b - cum_b)
        dec_f = jnp.exp(tot_f)
        decb_ref[c] = jnp.exp(tot_b)
        qdb_ref[rows, :] = _bf16(qd_b)

        vt = _bf16(p_ref[rows, _OFF_VB:_OFF_VB + B_WIDTH].T)
        ds_t = _dot(vt, _bf16(jnp.concatenate([ke_f, ke_b], axis=1)))

        for hd in range(B_HEADS):
            pair = slice((hd // 2) * LANES, (hd // 2 + 1) * LANES)
            vcols = slice(hd * B_VAL_DIM, (hd + 1) * B_VAL_DIM)
            mask = _head_lane_mask(hd)
            qf = _bf16(jnp.where(mask, qd_f[:, pair], 0.0))
            qb = _bf16(jnp.where(mask, qd_b[:, pair], 0.0))
            s_f = _dot_nt(qf, _bf16(ki_f[:, pair]))
            s_b = _dot_nt(qb, _bf16(ki_b[:, pair]))
            probs = _bf16(jnp.where(lower_incl, s_f, s_b))
            s_prev = sf_ref[hd]
            o = _dot(probs, vb[:, vcols]) + _dot_nt(qf, _bf16(s_prev))
            oa_ref[rows, vcols] = o
            sf_ref[hd] = s_prev * dec_f[:, pair] + ds_t[vcols, pair]
            dsb_ref[c, hd] = ds_t[vcols, QK_WIDTH + (hd // 2) * LANES:QK_WIDTH + (hd // 2 + 1) * LANES]

    for c in range(n_chunks):
        chunk_body(c)


def _mixer_fwd(x2d, gmix, win, ws, bs, lng, lnb, wg, bg, *, seq_len, tile_rows):
    t, d = x2d.shape
    n_tiles = t // tile_rows
    n_chunks = tile_rows // CHUNK
    kern = functools.partial(_mixer_fwd_kernel, tiles_per_seq=seq_len // tile_rows, n_chunks=n_chunks)
    const = lambda shape: pl.BlockSpec(shape, lambda i: (0,) * len(shape), pipeline_mode=pl.Buffered(1))
    rows = lambda width: pl.BlockSpec((tile_rows, width), lambda i: (i, 0))
    return pl.pallas_call(
        kern,
        grid=(n_tiles,),
        in_specs=[
            rows(d),
            const((1, d)),
            const((d, N_IN_PAD)),
            const((A_HEADS, CHUNK, CHUNK)),
            const((A_HEADS, CHUNK, 1)),
            const((1, A_WIDTH)),
            const((1, A_WIDTH)),
            const((LANES, 2 * QK_WIDTH)),
            const((1, 2 * QK_WIDTH)),
        ],
        out_specs=[
            rows(A_WIDTH),
            rows(B_WIDTH),
            rows(QK_WIDTH),
            rows(B_WIDTH),
            pl.BlockSpec((n_chunks, B_HEADS, B_VAL_DIM, LANES), lambda i: (i, 0, 0, 0)),
            pl.BlockSpec((n_chunks, 1, QK_WIDTH), lambda i: (i, 0, 0)),
        ],
        out_shape=[
            jax.ShapeDtypeStruct((t, A_WIDTH), jnp.bfloat16),
            jax.ShapeDtypeStruct((t, B_WIDTH), jnp.float32),
            jax.ShapeDtypeStruct((t, QK_WIDTH), jnp.bfloat16),
            jax.ShapeDtypeStruct((t, B_WIDTH), jnp.bfloat16),
            jax.ShapeDtypeStruct((t // CHUNK, B_HEADS, B_VAL_DIM, LANES), jnp.float32),
            jax.ShapeDtypeStruct((t // CHUNK, 1, QK_WIDTH), jnp.float32),
        ],
        scratch_shapes=[
            pltpu.VMEM((tile_rows, N_IN_PAD), jnp.float32),
            pltpu.VMEM((B_HEADS, B_VAL_DIM, LANES), jnp.float32),
        ],
        compiler_params=pltpu.CompilerParams(
            dimension_semantics=("arbitrary",), vmem_limit_bytes=VMEM_LIMIT_BYTES),
        name="mixer_fwd",
    )(x2d, gmix, win, ws, bs, lng, lnb, wg, bg)


def _mixer_bwd_kernel(x_ref, outa_ref, oa_ref, qdb_ref, gate_ref, dsb_ref, decb_ref, ggla_ref, wout_ref,
                      x1_ref, mixb_ref, sb_ref, *, tiles_per_seq, n_chunks):
    @pl.when(pl.program_id(0) % tiles_per_seq == 0)
    def _():
        sb_ref[...] = jnp.zeros_like(sb_ref)

    def chunk_body(step, carry):
        c = n_chunks - 1 - step
        r0 = pl.multiple_of(c * CHUNK, CHUNK)
        rows = pl.ds(r0, CHUNK)
        dec_b = decb_ref[c]
        for hd in range(B_HEADS):
            pair = slice((hd // 2) * LANES, (hd // 2 + 1) * LANES)
            vcols = slice(hd * B_VAL_DIM, (hd + 1) * B_VAL_DIM)
            qb = jnp.where(_head_lane_mask(hd), qdb_ref[rows, pair], jnp.zeros((), jnp.bfloat16))
            s_prev = sb_ref[hd]
            o = oa_ref[rows, vcols] + _dot_nt(qb, _bf16(s_prev))
            out_b = o * _rms_scale(o) * ggla_ref[:, vcols] * gate_ref[rows, vcols].astype(jnp.float32)
            mixb_ref[rows, vcols] = _bf16(out_b)
            sb_ref[hd] = s_prev * dec_b[:, pair] + dsb_ref[c, hd]
        return carry

    lax.fori_loop(0, n_chunks, chunk_body, 0)
    x1_ref[...] = (x_ref[...]
                   + _dot(outa_ref[...], wout_ref[:A_WIDTH, :])
                   + _dot(mixb_ref[...], wout_ref[A_WIDTH:, :]))


def _mixer_bwd(x2d, outa, oa, qdb, gate, dsb, decb, ggla, wout, *, seq_len, tile_rows):
    t, d = x2d.shape
    n_tiles = t // tile_rows
    n_chunks = tile_rows // CHUNK
    kern = functools.partial(_mixer_bwd_kernel, tiles_per_seq=seq_len // tile_rows, n_chunks=n_chunks)
    const = lambda shape: pl.BlockSpec(shape, lambda i: (0,) * len(shape), pipeline_mode=pl.Buffered(1))
    rows = lambda width: pl.BlockSpec((tile_rows, width), lambda i: (n_tiles - 1 - i, 0))
    return pl.pallas_call(
        kern,
        grid=(n_tiles,),
        in_specs=[
            rows(d),
            rows(A_WIDTH),
            rows(B_WIDTH),
            rows(QK_WIDTH),
            rows(B_WIDTH),
            pl.BlockSpec((n_chunks, B_HEADS, B_VAL_DIM, LANES), lambda i: (n_tiles - 1 - i, 0, 0, 0)),
            pl.BlockSpec((n_chunks, 1, QK_WIDTH), lambda i: (n_tiles - 1 - i, 0, 0)),
            const((1, B_WIDTH)),
            const((A_WIDTH + B_WIDTH, d)),
        ],
        out_specs=rows(d),
        out_shape=jax.ShapeDtypeStruct((t, d), jnp.float32),
        scratch_shapes=[
            pltpu.VMEM((tile_rows, B_WIDTH), jnp.bfloat16),
            pltpu.VMEM((B_HEADS, B_VAL_DIM, LANES), jnp.float32),
        ],
        compiler_params=pltpu.CompilerParams(
            dimension_semantics=("arbitrary",), vmem_limit_bytes=VMEM_LIMIT_BYTES),
        name="mixer_bwd",
    )(x2d, outa, oa, qdb, gate, dsb, decb, ggla, wout)


def _ffn_kernel(x_ref, xprev_ref, xnext_ref, gffn_ref, wup_ref, cw_ref, cb_ref, wdown_ref, gfin_ref,
                out_ref, acc_ref, *, tiles_per_seq, d_ff, apply_final_norm):
    tile_rows = x_ref.shape[0]
    pos = pl.program_id(0) % tiles_per_seq
    g = gffn_ref[...]

    def normed(xv):
        return xv * _rms_scale(xv) * g

    x = x_ref[...]
    h_prev = jnp.where(pos == 0, 0.0, normed(xprev_ref[...]))
    h_next = jnp.where(pos == tiles_per_seq - 1, 0.0, normed(xnext_ref[...]))
    hcat = _bf16(jnp.concatenate([h_prev, normed(x), h_next], axis=0))
    n_cat = tile_rows + 2 * HALO_ROWS
    centre = slice(HALO_ROWS, HALO_ROWS + tile_rows)

    def conv(u, cols):
        w = cw_ref[:, cols]
        u_dn = pltpu.roll(u, 1, 0)
        u_up = pltpu.roll(u, n_cat - 1, 0)
        return (cb_ref[:, cols] + u_dn[centre] * w[0:1] + u[centre] * w[1:2] + u_up[centre] * w[2:3])

    n_blocks = d_ff // FF_BLOCK
    gate_cols = lambda j: slice(j * FF_BLOCK, (j + 1) * FF_BLOCK)
    val_cols = lambda j: slice(d_ff + j * FF_BLOCK, d_ff + (j + 1) * FF_BLOCK)
    up = lambda j: (_dot(hcat, wup_ref[:, gate_cols(j)]), _dot(hcat, wup_ref[:, val_cols(j)]))

    u_next = up(0)
    for j in range(n_blocks):
        ug, uv = u_next
        if j + 1 < n_blocks:
            u_next = up(j + 1)
        zg = conv(ug, gate_cols(j))
        zv = conv(uv, val_cols(j))
        act = _bf16(zg * jax.nn.sigmoid(zg) * zv)
        contrib = _dot(act, wdown_ref[gate_cols(j), :])
        if j == 0:
            acc_ref[...] = x + contrib
        else:
            acc_ref[...] += contrib

    y = acc_ref[...]
    if apply_final_norm:
        y = y * _rms_scale(y) * gfin_ref[...]
    out_ref[...] = y


def _ffn(x2d, gffn, wup, cw, cb, wdown, gfin, *, seq_len, tile_rows, apply_final_norm):
    t, d = x2d.shape
    d_ff = wdown.shape[0]
    assert d_ff % FF_BLOCK == 0
    n_tiles = t // tile_rows
    blocks_per_tile = tile_rows // HALO_ROWS
    n_row_blocks = t // HALO_ROWS
    kern = functools.partial(_ffn_kernel, tiles_per_seq=seq_len // tile_rows, d_ff=d_ff,
                             apply_final_norm=apply_final_norm)
    const = lambda shape: pl.BlockSpec(shape, lambda i: (0,) * len(shape), pipeline_mode=pl.Buffered(1))
    return pl.pallas_call(
        kern,
        grid=(n_tiles,),
        in_specs=[
            pl.BlockSpec((tile_rows, d), lambda i: (i, 0)),
            pl.BlockSpec((HALO_ROWS, d), lambda i: (jnp.maximum(i * blocks_per_tile - 1, 0), 0)),
            pl.BlockSpec((HALO_ROWS, d), lambda i: (jnp.minimum((i + 1) * blocks_per_tile, n_row_blocks - 1), 0)),
            const((1, d)),
            const((d, 2 * d_ff)),
            const((CONV_WIDTH, 2 * d_ff)),
            const((1, 2 * d_ff)),
            const((d_ff, d)),
            const((1, d)),
        ],
        out_specs=pl.BlockSpec((tile_rows, d), lambda i: (i, 0)),
        out_shape=jax.ShapeDtypeStruct((t, d), jnp.float32),
        scratch_shapes=[pltpu.VMEM((tile_rows, d), jnp.float32)],
        compiler_params=pltpu.CompilerParams(
            dimension_semantics=("parallel",), vmem_limit_bytes=VMEM_LIMIT_BYTES),
        name="ffn_final" if apply_final_norm else "ffn",
    )(x2d, x2d, x2d, gffn, wup, cw, cb, wdown, gfin)


def _pick_tile_rows(seq_len, target):
    tile = min(target, seq_len)
    assert seq_len % tile == 0 and tile % CHUNK == 0
    return tile


def kernel(x, g_mix, w_in, w_s, b_s, ln_g, ln_b, w_gate_f, b_gate_f, w_gate_b, b_gate_b, g_gla, w_out, g_ffn, w_up, conv_w, conv_b, w_down, g_final):
    bsz, seq_len, d = x.shape
    depth = w_in.shape[0]
    assert seq_len % CHUNK == 0
    tile_rows = _pick_tile_rows(seq_len, 512)
    f32, bf16 = jnp.float32, jnp.bfloat16

    n_in = w_in.shape[-1]
    win = jnp.pad(w_in, ((0, 0), (0, 0), (0, N_IN_PAD - n_in))).astype(bf16)
    wg = jnp.zeros((depth, LANES, 2 * QK_WIDTH), f32)
    wg = wg.at[:, :GATE_RANK, :QK_WIDTH].set(w_gate_f)
    wg = wg.at[:, GATE_RANK:2 * GATE_RANK, QK_WIDTH:].set(w_gate_b)
    wg = wg.astype(bf16)
    bg = jnp.concatenate([b_gate_f, b_gate_b], axis=-1)[:, None, :]
    ws = w_s.astype(bf16)
    bs = b_s[..., None]
    lng = ln_g.reshape(depth, 1, A_WIDTH)
    lnb = ln_b.reshape(depth, 1, A_WIDTH)
    ggla = g_gla.reshape(depth, 1, B_WIDTH)
    wout = w_out.astype(bf16)
    wup = w_up.astype(bf16)
    wdown = w_down.astype(bf16)
    gfin = g_final.reshape(1, d)

    xs = x.reshape(bsz * seq_len, d)
    for l in range(depth):
        outa, oa, qdb, gate, dsb, decb = _mixer_fwd(
            xs, g_mix[l][None], win[l], ws[l], bs[l], lng[l], lnb[l], wg[l], bg[l],
            seq_len=seq_len, tile_rows=tile_rows)
        xs = _mixer_bwd(xs, outa, oa, qdb, gate, dsb, decb, ggla[l], wout[l],
                        seq_len=seq_len, tile_rows=tile_rows)
        xs = _ffn(xs, g_ffn[l][None], wup[l], conv_w[l], conv_b[l][None], wdown[l], gfin,
                  seq_len=seq_len, tile_rows=tile_rows, apply_final_norm=(l == depth - 1))
    return xs.reshape(bsz, seq_len, d)
```

```python
import functools

import jax
import jax.numpy as jnp
from jax import lax
from jax.experimental import pallas as pl
from jax.experimental.pallas import tpu as pltpu

CHUNK = 128
A_HEADS = 4
A_HEAD_DIM = 128
B_HEADS = 4
B_KEY_DIM = 64
B_VAL_DIM = 128
GATE_RANK = 16
GATE_NORMALIZER = 16.0
CONV_WIDTH = 3
EPS = 1e-6

A_WIDTH = A_HEADS * A_HEAD_DIM
QK_WIDTH = B_HEADS * B_KEY_DIM
B_WIDTH = B_HEADS * B_VAL_DIM

LANES = 128
HALO_ROWS = 8
VMEM_LIMIT_BYTES = 56 * 1024 * 1024

_OFF_U = 0
_OFF_V = _OFF_U + A_WIDTH
_OFF_Q = _OFF_V + A_WIDTH
_OFF_K = _OFF_Q + QK_WIDTH
_OFF_VB = _OFF_K + QK_WIDTH
_OFF_G = _OFF_VB + B_WIDTH
_OFF_R = _OFF_G + B_WIDTH
N_IN_PAD = _OFF_R + LANES

FF_BLOCK = 256
PROJ_ROWS = 512

_NT = (((1,), (1,)), ((), ()))


def _dot(a, b):
    return jnp.dot(a, b, preferred_element_type=jnp.float32)


def _dot_nt(a, b):
    return lax.dot_general(a, b, _NT, preferred_element_type=jnp.float32)


def _bf16(a):
    return a.astype(jnp.bfloat16)


def _rms_scale(x):
    return lax.rsqrt(jnp.mean(x * x, axis=-1, keepdims=True) + EPS)


def _gelu(x):
    return 0.5 * x * (1.0 + lax.erf(x * (2.0 ** -0.5)))


def _log_sigmoid(x):
    return jnp.minimum(x, 0.0) - jnp.log(1.0 + jnp.exp(-jnp.abs(x)))


def _split_hi_lo(a):
    hi = _bf16(a)
    lo = _bf16(a - hi.astype(jnp.float32))
    return hi, lo


def _head_lane_mask(h):
    lane = lax.broadcasted_iota(jnp.int32, (CHUNK, LANES), 1)
    return (lane >= B_KEY_DIM) if (h % 2) else (lane < B_KEY_DIM)


def _mixer_fwd_kernel(x_ref, gmix_ref, win_ref, ws_ref, bs_ref, lng_ref, lnb_ref, wg_ref, bg_ref,
                      outa_ref, oa_ref, qdb_ref, gate_ref, dsb_ref, decb_ref,
                      p_ref, sf_ref, *, tiles_per_seq, n_chunks):
    @pl.when(pl.program_id(0) % tiles_per_seq == 0)
    def _():
        sf_ref[...] = jnp.zeros_like(sf_ref)

    for r0 in range(0, x_ref.shape[0], PROJ_ROWS):
        x = x_ref[pl.ds(r0, PROJ_ROWS), :]
        h = _bf16(x * _rms_scale(x) * gmix_ref[...])
        p_ref[pl.ds(r0, PROJ_ROWS), :] = _dot(h, win_ref[...])

    row = lax.broadcasted_iota(jnp.int32, (CHUNK, CHUNK), 0)
    col = lax.broadcasted_iota(jnp.int32, (CHUNK, CHUNK), 1)
    lower_incl = row >= col
    tril = _bf16(jnp.where(lower_incl, 1.0, 0.0))
    triu = _bf16(jnp.where(row <= col, 1.0, 0.0))
    q_scale = B_KEY_DIM ** -0.5

    def chunk_body(c):
        rows = pl.ds(c * CHUNK, CHUNK)

        for hd in range(A_HEADS):
            cols = slice(hd * A_HEAD_DIM, (hd + 1) * A_HEAD_DIM)
            u = _gelu(p_ref[rows, _OFF_U + hd * A_HEAD_DIM:_OFF_U + (hd + 1) * A_HEAD_DIM])
            v = _gelu(p_ref[rows, _OFF_V + hd * A_HEAD_DIM:_OFF_V + (hd + 1) * A_HEAD_DIM])
            mu = jnp.mean(v, axis=-1, keepdims=True)
            d = v - mu
            var = jnp.mean(d * d, axis=-1, keepdims=True)
            vn = d * lax.rsqrt(var + EPS) * lng_ref[:, cols] + lnb_ref[:, cols]
            mixed = _dot(ws_ref[hd], _bf16(vn)) + bs_ref[hd]
            outa_ref[rows, cols] = _bf16(u * mixed)

        q = p_ref[rows, _OFF_Q:_OFF_Q + QK_WIDTH] * q_scale
        k = p_ref[rows, _OFF_K:_OFF_K + QK_WIDTH]
        vb = _bf16(p_ref[rows, _OFF_VB:_OFF_VB + B_WIDTH])
        pg = p_ref[rows, _OFF_G:_OFF_G + B_WIDTH]
        r = _bf16(p_ref[rows, _OFF_R:_OFF_R + LANES])
        gate_ref[rows, :] = _bf16(pg * jax.nn.sigmoid(pg))

        la = _log_sigmoid(_dot(r, wg_ref[...]) + bg_ref[...]) * (1.0 / GATE_NORMALIZER)
        la_f_hi, la_f_lo = _split_hi_lo(la[:, :QK_WIDTH])
        la_b_hi, la_b_lo = _split_hi_lo(la[:, QK_WIDTH:])
        cum_f = _dot(tril, la_f_hi) + _dot(tril, la_f_lo)
        cum_b = _dot(triu, la_b_hi) + _dot(triu, la_b_lo)
        tot_f = cum_f[CHUNK - 1:CHUNK, :]
        tot_b = cum_b[0:1, :]
        qd_f = q * jnp.exp(cum_f)
        ki_f = k * jnp.exp(-cum_f)
        ke_f = k * jnp.exp(tot_f - cum_f)
        qd_b = q * jnp.exp(cum_b)
        ki_b = k * jnp.exp(-cum_b)
        ke_b = k * jnp.exp(tot_b - cum_b)
        dec_f = jnp.exp(tot_f)
        decb_ref[c] = jnp.exp(tot_b)
        qdb_ref[rows, :] = _bf16(qd_b)

        vt = _bf16(p_ref[rows, _OFF_VB:_OFF_VB + B_WIDTH].T)
        ds_t = _dot(vt, _bf16(jnp.concatenate([ke_f, ke_b], axis=1)))

        for hd in range(B_HEADS):
            pair = slice((hd // 2) * LANES, (hd // 2 + 1) * LANES)
            vcols = slice(hd * B_VAL_DIM, (hd + 1) * B_VAL_DIM)
            mask = _head_lane_mask(hd)
            qf = _bf16(jnp.where(mask, qd_f[:, pair], 0.0))
            qb = _bf16(jnp.where(mask, qd_b[:, pair], 0.0))
            s_f = _dot_nt(qf, _bf16(ki_f[:, pair]))
            s_b = _dot_nt(qb, _bf16(ki_b[:, pair]))
            probs = _bf16(jnp.where(lower_incl, s_f, s_b))
            s_prev = sf_ref[hd]
            o = _dot(probs, vb[:, vcols]) + _dot_nt(qf, _bf16(s_prev))
            oa_ref[rows, vcols] = o
            sf_ref[hd] = s_prev * dec_f[:, pair] + ds_t[vcols, pair]
            dsb_ref[c, hd] = ds_t[vcols, QK_WIDTH + (hd // 2) * LANES:QK_WIDTH + (hd // 2 + 1) * LANES]

    for c in range(n_chunks):
        chunk_body(c)


def _mixer_fwd(x2d, gmix, win, ws, bs, lng, lnb, wg, bg, *, seq_len, tile_rows):
    t, d = x2d.shape
    n_tiles = t // tile_rows
    n_chunks = tile_rows // CHUNK
    kern = functools.partial(_mixer_fwd_kernel, tiles_per_seq=seq_len // tile_rows, n_chunks=n_chunks)
    const = lambda shape: pl.BlockSpec(shape, lambda i: (0,) * len(shape), pipeline_mode=pl.Buffered(1))
    rows = lambda width: pl.BlockSpec((tile_rows, width), lambda i: (i, 0))
    return pl.pallas_call(
        kern,
        grid=(n_tiles,),
        in_specs=[
            rows(d),
            const((1, d)),
            const((d, N_IN_PAD)),
            const((A_HEADS, CHUNK, CHUNK)),
            const((A_HEADS, CHUNK, 1)),
            const((1, A_WIDTH)),
            const((1, A_WIDTH)),
            const((LANES, 2 * QK_WIDTH)),
            const((1, 2 * QK_WIDTH)),
        ],
        out_specs=[
            rows(A_WIDTH),
            rows(B_WIDTH),
            rows(QK_WIDTH),
            rows(B_WIDTH),
            pl.BlockSpec((n_chunks, B_HEADS, B_VAL_DIM, LANES), lambda i: (i, 0, 0, 0)),
            pl.BlockSpec((n_chunks, 1, QK_WIDTH), lambda i: (i, 0, 0)),
        ],
        out_shape=[
            jax.ShapeDtypeStruct((t, A_WIDTH), jnp.bfloat16),
            jax.ShapeDtypeStruct((t, B_WIDTH), jnp.float32),
            jax.ShapeDtypeStruct((t, QK_WIDTH), jnp.bfloat16),
            jax.ShapeDtypeStruct((t, B_WIDTH), jnp.bfloat16),
            jax.ShapeDtypeStruct((t // CHUNK, B_HEADS, B_VAL_DIM, LANES), jnp.float32),
            jax.ShapeDtypeStruct((t // CHUNK, 1, QK_WIDTH), jnp.float32),
        ],
        scratch_shapes=[
            pltpu.VMEM((tile_rows, N_IN_PAD), jnp.float32),
            pltpu.VMEM((B_HEADS, B_VAL_DIM, LANES), jnp.float32),
        ],
        compiler_params=pltpu.CompilerParams(
            dimension_semantics=("arbitrary",), vmem_limit_bytes=VMEM_LIMIT_BYTES),
        name="mixer_fwd",
    )(x2d, gmix, win, ws, bs, lng, lnb, wg, bg)


def _mixer_bwd_kernel(x_ref, outa_ref, oa_ref, qdb_ref, gate_ref, dsb_ref, decb_ref, ggla_ref, wout_ref,
                      x1_ref, mixb_ref, sb_ref, *, tiles_per_seq, n_chunks):
    @pl.when(pl.program_id(0) % tiles_per_seq == 0)
    def _():
        sb_ref[...] = jnp.zeros_like(sb_ref)

    for c in reversed(range(n_chunks)):
        rows = pl.ds(c * CHUNK, CHUNK)
        dec_b = decb_ref[c]
        for hd in range(B_HEADS):
            pair = slice((hd // 2) * LANES, (hd // 2 + 1) * LANES)
            vcols = slice(hd * B_VAL_DIM, (hd + 1) * B_VAL_DIM)
            qb = jnp.where(_head_lane_mask(hd), qdb_ref[rows, pair], jnp.zeros((), jnp.bfloat16))
            s_prev = sb_ref[hd]
            o = oa_ref[rows, vcols] + _dot_nt(qb, _bf16(s_prev))
            out_b = o * _rms_scale(o) * ggla_ref[:, vcols] * gate_ref[rows, vcols].astype(jnp.float32)
            mixb_ref[rows, vcols] = _bf16(out_b)
            sb_ref[hd] = s_prev * dec_b[:, pair] + dsb_ref[c, hd]

    x1_ref[...] = (x_ref[...]
                   + _dot(outa_ref[...], wout_ref[:A_WIDTH, :])
                   + _dot(mixb_ref[...], wout_ref[A_WIDTH:, :]))


def _mixer_bwd(x2d, outa, oa, qdb, gate, dsb, decb, ggla, wout, *, seq_len, tile_rows):
    t, d = x2d.shape
    n_tiles = t // tile_rows
    n_chunks = tile_rows // CHUNK
    kern = functools.partial(_mixer_bwd_kernel, tiles_per_seq=seq_len // tile_rows, n_chunks=n_chunks)
    const = lambda shape: pl.BlockSpec(shape, lambda i: (0,) * len(shape), pipeline_mode=pl.Buffered(1))
    rows = lambda width: pl.BlockSpec((tile_rows, width), lambda i: (n_tiles - 1 - i, 0))
    return pl.pallas_call(
        kern,
        grid=(n_tiles,),
        in_specs=[
            rows(d),
            rows(A_WIDTH),
            rows(B_WIDTH),
            rows(QK_WIDTH),
            rows(B_WIDTH),
            pl.BlockSpec((n_chunks, B_HEADS, B_VAL_DIM, LANES), lambda i: (n_tiles - 1 - i, 0, 0, 0)),
            pl.BlockSpec((n_chunks, 1, QK_WIDTH), lambda i: (n_tiles - 1 - i, 0, 0)),
            const((1, B_WIDTH)),
            const((A_WIDTH + B_WIDTH, d)),
        ],
        out_specs=rows(d),
        out_shape=jax.ShapeDtypeStruct((t, d), jnp.float32),
        scratch_shapes=[
            pltpu.VMEM((tile_rows, B_WIDTH), jnp.bfloat16),
            pltpu.VMEM((B_HEADS, B_VAL_DIM, LANES), jnp.float32),
        ],
        compiler_params=pltpu.CompilerParams(
            dimension_semantics=("arbitrary",), vmem_limit_bytes=VMEM_LIMIT_BYTES),
        name="mixer_bwd",
    )(x2d, outa, oa, qdb, gate, dsb, decb, ggla, wout)


def _ffn_kernel(x_ref, xprev_ref, xnext_ref, gffn_ref, wup_ref, cw_ref, cb_ref, wdown_ref, gfin_ref,
                out_ref, acc_ref, *, tiles_per_seq, d_ff, apply_final_norm):
    tile_rows = x_ref.shape[0]
    pos = pl.program_id(0) % tiles_per_seq
    g = gffn_ref[...]

    def normed(xv):
        return xv * _rms_scale(xv) * g

    x = x_ref[...]
    h_prev = jnp.where(pos == 0, 0.0, normed(xprev_ref[...]))
    h_next = jnp.where(pos == tiles_per_seq - 1, 0.0, normed(xnext_ref[...]))
    hcat = _bf16(jnp.concatenate([h_prev, normed(x), h_next], axis=0))
    n_cat = tile_rows + 2 * HALO_ROWS
    centre = slice(HALO_ROWS, HALO_ROWS + tile_rows)

    def conv(u, cols):
        w = cw_ref[:, cols]
        u_dn = pltpu.roll(u, 1, 0)
        u_up = pltpu.roll(u, n_cat - 1, 0)
        return (cb_ref[:, cols] + u_dn[centre] * w[0:1] + u[centre] * w[1:2] + u_up[centre] * w[2:3])

    n_blocks = d_ff // FF_BLOCK
    gate_cols = lambda j: slice(j * FF_BLOCK, (j + 1) * FF_BLOCK)
    val_cols = lambda j: slice(d_ff + j * FF_BLOCK, d_ff + (j + 1) * FF_BLOCK)
    up = lambda j: (_dot(hcat, wup_ref[:, gate_cols(j)]), _dot(hcat, wup_ref[:, val_cols(j)]))

    u_next = up(0)
    for j in range(n_blocks):
        ug, uv = u_next
        if j + 1 < n_blocks:
            u_next = up(j + 1)
        zg = conv(ug, gate_cols(j))
        zv = conv(uv, val_cols(j))
        act = _bf16(zg * jax.nn.sigmoid(zg) * zv)
        contrib = _dot(act, wdown_ref[gate_cols(j), :])
        if j == 0:
            acc_ref[...] = x + contrib
        else:
            acc_ref[...] += contrib

    y = acc_ref[...]
    if apply_final_norm:
        y = y * _rms_scale(y) * gfin_ref[...]
    out_ref[...] = y


def _ffn(x2d, gffn, wup, cw, cb, wdown, gfin, *, seq_len, tile_rows, apply_final_norm):
    t, d = x2d.shape
    d_ff = wdown.shape[0]
    assert d_ff % FF_BLOCK == 0
    n_tiles = t // tile_rows
    blocks_per_tile = tile_rows // HALO_ROWS
    n_row_blocks = t // HALO_ROWS
    kern = functools.partial(_ffn_kernel, tiles_per_seq=seq_len // tile_rows, d_ff=d_ff,
                             apply_final_norm=apply_final_norm)
    const = lambda shape: pl.BlockSpec(shape, lambda i: (0,) * len(shape), pipeline_mode=pl.Buffered(1))
    return pl.pallas_call(
        kern,
        grid=(n_tiles,),
        in_specs=[
            pl.BlockSpec((tile_rows, d), lambda i: (i, 0)),
            pl.BlockSpec((HALO_ROWS, d), lambda i: (jnp.maximum(i * blocks_per_tile - 1, 0), 0)),
            pl.BlockSpec((HALO_ROWS, d), lambda i: (jnp.minimum((i + 1) * blocks_per_tile, n_row_blocks - 1), 0)),
            const((1, d)),
            const((d, 2 * d_ff)),
            const((CONV_WIDTH, 2 * d_ff)),
            const((1, 2 * d_ff)),
            const((d_ff, d)),
            const((1, d)),
        ],
        out_specs=pl.BlockSpec((tile_rows, d), lambda i: (i, 0)),
        out_shape=jax.ShapeDtypeStruct((t, d), jnp.float32),
        scratch_shapes=[pltpu.VMEM((tile_rows, d), jnp.float32)],
        compiler_params=pltpu.CompilerParams(
            dimension_semantics=("parallel",), vmem_limit_bytes=VMEM_LIMIT_BYTES),
        name="ffn_final" if apply_final_norm else "ffn",
    )(x2d, x2d, x2d, gffn, wup, cw, cb, wdown, gfin)


def _pick_tile_rows(seq_len, target):
    tile = min(target, seq_len)
    assert seq_len % tile == 0 and tile % CHUNK == 0
    return tile


def kernel(x, g_mix, w_in, w_s, b_s, ln_g, ln_b, w_gate_f, b_gate_f, w_gate_b, b_gate_b, g_gla, w_out, g_ffn, w_up, conv_w, conv_b, w_down, g_final):
    bsz, seq_len, d = x.shape
    depth = w_in.shape[0]
    assert seq_len % CHUNK == 0
    tile_rows = _pick_tile_rows(seq_len, 512)
    fwd_tile_rows = _pick_tile_rows(seq_len, 2 * PROJ_ROWS)
    assert fwd_tile_rows % PROJ_ROWS == 0
    f32, bf16 = jnp.float32, jnp.bfloat16

    n_in = w_in.shape[-1]
    win = jnp.pad(w_in, ((0, 0), (0, 0), (0, N_IN_PAD - n_in))).astype(bf16)
    wg = jnp.zeros((depth, LANES, 2 * QK_WIDTH), f32)
    wg = wg.at[:, :GATE_RANK, :QK_WIDTH].set(w_gate_f)
    wg = wg.at[:, GATE_RANK:2 * GATE_RANK, QK_WIDTH:].set(w_gate_b)
    wg = wg.astype(bf16)
    bg = jnp.concatenate([b_gate_f, b_gate_b], axis=-1)[:, None, :]
    ws = w_s.astype(bf16)
    bs = b_s[..., None]
    lng = ln_g.reshape(depth, 1, A_WIDTH)
    lnb = ln_b.reshape(depth, 1, A_WIDTH)
    ggla = g_gla.reshape(depth, 1, B_WIDTH)
    wout = w_out.astype(bf16)
    wup = w_up.astype(bf16)
    wdown = w_down.astype(bf16)
    gfin = g_final.reshape(1, d)

    xs = x.reshape(bsz * seq_len, d)
    for l in range(depth):
        outa, oa, qdb, gate, dsb, decb = _mixer_fwd(
            xs, g_mix[l][None], win[l], ws[l], bs[l], lng[l], lnb[l], wg[l], bg[l],
            seq_len=seq_len, tile_rows=fwd_tile_rows)
        xs = _mixer_bwd(xs, outa, oa, qdb, gate, dsb, decb, ggla[l], wout[l],
                        seq_len=seq_len, tile_rows=tile_rows)
        xs = _ffn(xs, g_ffn[l][None], wup[l], conv_w[l], conv_b[l][None], wdown[l], gfin,
                  seq_len=seq_len, tile_rows=tile_rows, apply_final_norm=(l == depth - 1))
    return xs.reshape(bsz, seq_len, d)
```
